```python
import math
import jax, jax.numpy as jnp
from jax import lax
import numpy as np

D_MODEL = 1024
BATCH = 8
SEQ = 4096
DEPTH = 2

CHUNK = 64
N_EVEN = (DEPTH + 1) // 2
N_ODD = DEPTH // 2
NORM_EPS = 1e-6

RW_WIDTH = D_MODEL // 2
RW_HEAD = 64
RW_HEADS = RW_WIDTH // RW_HEAD
RW_DECAY_LORA = 64
RW_AAA_LORA = 64
RW_GATE_LORA = 128
RW_COLS = 3 * RW_WIDTH + RW_DECAY_LORA + RW_AAA_LORA + RW_GATE_LORA
RW_LN_EPS = 64e-5

MB_WIDTH = D_MODEL // 2
MB_HEAD = 64
MB_HEADS = MB_WIDTH // MB_HEAD
MB_GROUPS = 2
MB_STATE = 128
MB_CONV = 4
MB_CONV_CH = MB_WIDTH + 2 * MB_GROUPS * MB_STATE
MB_COLS = MB_WIDTH + MB_CONV_CH + MB_HEADS
MB_NORM_EPS = 1e-5

EVEN_COLS = RW_COLS + MB_COLS
EVEN_OUT = RW_WIDTH + MB_WIDTH

ML_HEADS = 8
ML_V_WIDTH = D_MODEL
ML_QK_WIDTH = D_MODEL // 2
ML_DV = ML_V_WIDTH // ML_HEADS
ML_DQK = ML_QK_WIDTH // ML_HEADS
ML_GATE_CAP = 15.0
ODD_COLS = 2 * ML_QK_WIDTH + 2 * ML_V_WIDTH + 2 * ML_HEADS

N_MEM = 256
XA_HEADS = 4
XA_HEAD = D_MODEL // XA_HEADS
D_FF = 4 * D_MODEL

kernel_name = 'hybrid_rwkv7_mamba2_mlstm_trunk'

F32 = jnp.float32


def rmsnorm(x, g, eps=NORM_EPS):
    x32 = x.astype(F32)
    y = x32 * lax.rsqrt(jnp.mean(x32 * x32, axis=-1, keepdims=True) + eps)
    return (y * g.astype(F32)).astype(x.dtype)


def token_shift(x):
    return jnp.pad(x, ((0, 0), (1, 0), (0, 0)))[:, :-1]


def causal_depthwise_conv(x, w, b):
    K = w.shape[0]
    y = lax.conv_general_dilated(x, w[:, None, :], window_strides=(1,), padding=[(K - 1, 0)],
                                 dimension_numbers=('NWC', 'WIO', 'NWC'),
                                 feature_group_count=x.shape[-1])
    return y + b


def softcap(x, cap=ML_GATE_CAP):
    return cap * jnp.tanh(x / cap)


def rwkv7_scan(r, w, k, v, a, b):
    def step(S, inp):
        r_t, w_t, k_t, v_t, a_t, b_t = inp
        sa = jnp.einsum('bhvk,bhk->bhv', S, a_t)
        S = S * w_t[:, :, None, :] + sa[..., None] * b_t[:, :, None, :] + v_t[..., None] * k_t[:, :, None, :]
        return S, jnp.einsum('bhvk,bhk->bhv', S, r_t)
    Bsz, T, H, N = r.shape
    xs = tuple(jnp.moveaxis(t, 1, 0) for t in (r, w, k, v, a, b))
    _, y = lax.scan(step, jnp.zeros((Bsz, H, N, N), F32), xs)
    return jnp.moveaxis(y, 0, 1)


def rwkv7_mix(p, mu, w0, w2, a0, a2, g2, k_k, k_a, r_k, ln_w, ln_b):
    dt_in = p.dtype
    p = p.astype(F32)
    Bsz, T, _ = p.shape
    p = p + (token_shift(p) - p) * mu.astype(F32)
    r, k, v, wd, ad, gd = jnp.split(p, [RW_WIDTH, 2 * RW_WIDTH, 3 * RW_WIDTH,
                                        3 * RW_WIDTH + RW_DECAY_LORA,
                                        3 * RW_WIDTH + RW_DECAY_LORA + RW_AAA_LORA], axis=-1)
    w_log = -jax.nn.softplus(-(w0.astype(F32) + jnp.tanh(wd) @ w2.astype(F32))) - 0.5
    decay = jnp.exp(-jnp.exp(w_log))
    a = jax.nn.sigmoid(a0.astype(F32) + ad @ a2.astype(F32))
    g = jax.nn.sigmoid(gd) @ g2.astype(F32)
    hd = lambda t: t.reshape(Bsz, T, RW_HEADS, RW_HEAD)
    kk = hd(k * k_k.astype(F32))
    kk = kk / jnp.maximum(jnp.sqrt(jnp.sum(kk * kk, axis=-1, keepdims=True)), 1e-12)
    k = k * (1.0 + (a - 1.0) * k_a.astype(F32))
    r_h, k_h, v_h, a_h = hd(r), hd(k), hd(v), hd(a)
    y = rwkv7_scan(r_h, hd(decay), k_h, v_h, -kk, kk * a_h)
    mean = jnp.mean(y, axis=-1, keepdims=True)
    var = jnp.mean(jnp.square(y - mean), axis=-1, keepdims=True)
    y = (y - mean) * lax.rsqrt(var + RW_LN_EPS) * ln_w.astype(F32).reshape(RW_HEADS, RW_HEAD) \
        + ln_b.astype(F32).reshape(RW_HEADS, RW_HEAD)
    y = y + jnp.sum(r_h * k_h * r_k.astype(F32), axis=-1, keepdims=True) * v_h
    return (y.reshape(Bsz, T, RW_WIDTH) * g).astype(dt_in)


def ssd_chunked(x, dt, A, Bm, Cm):
    Bsz, T, H, P = x.shape
    G, N = Bm.shape[2], Bm.shape[3]
    R = H // G
    nc = T // CHUNK
    x = x.reshape(Bsz, nc, CHUNK, G, R, P)
    dt = dt.reshape(Bsz, nc, CHUNK, G, R)
    Bm = Bm.reshape(Bsz, nc, CHUNK, G, N)
    Cm = Cm.reshape(Bsz, nc, CHUNK, G, N)
    acum = jnp.cumsum(dt * A.reshape(G, R), axis=2)
    acum_t = jnp.moveaxis(acum, 2, -1)
    seg = acum_t[..., :, None] - acum_t[..., None, :]
    causal = jnp.tril(jnp.ones((CHUNK, CHUNK), dtype=bool))
    Lmat = jnp.exp(jnp.where(causal, seg, -jnp.inf))
    xdt = x * dt[..., None]
    CB = jnp.einsum('bclgn,bcsgn->bcgls', Cm, Bm)
    y_diag = jnp.einsum('bcgrls,bcsgrp->bclgrp', CB[:, :, :, None] * Lmat, xdt)
    decay_to_end = jnp.exp(acum[:, :, -1:] - acum)
    chunk_states = jnp.einsum('bclgn,bclgr,bclgrp->bcgrpn', Bm, decay_to_end, xdt)
    chunk_decay = jnp.exp(acum[:, :, -1])

    def step(S, inp):
        st, dec = inp
        return S * dec[..., None, None] + st, S
    _, S_prev = lax.scan(step, jnp.zeros((Bsz, G, R, P, N), F32),
                         (jnp.moveaxis(chunk_states, 1, 0), jnp.moveaxis(chunk_decay, 1, 0)))
    S_prev = jnp.moveaxis(S_prev, 0, 1)
    y_off = jnp.einsum('bclgn,bcgrpn,bclgr->bclgrp', Cm, S_prev, jnp.exp(acum))
    return (y_diag + y_off).reshape(Bsz, T, H, P)


def mamba2_mix(p, conv_w, conv_b, dt_bias, A_log, D_skip, norm_w):
    dt_in = p.dtype
    p = p.astype(F32)
    Bsz, T, _ = p.shape
    z, xbc, dt = jnp.split(p, [MB_WIDTH, MB_WIDTH + MB_CONV_CH], axis=-1)
    xbc = jax.nn.silu(causal_depthwise_conv(xbc, conv_w.astype(F32), conv_b.astype(F32)))
    xs, Bm, Cm = jnp.split(xbc, [MB_WIDTH, MB_WIDTH + MB_GROUPS * MB_STATE], axis=-1)
    xs = xs.reshape(Bsz, T, MB_HEADS, MB_HEAD)
    dt = jax.nn.softplus(dt + dt_bias.astype(F32))
    A = -jnp.exp(A_log.astype(F32))
    y = ssd_chunked(xs, dt, A, Bm.reshape(Bsz, T, MB_GROUPS, MB_STATE),
                    Cm.reshape(Bsz, T, MB_GROUPS, MB_STATE))
    y = y + D_skip.astype(F32)[:, None] * xs
    y = (y.reshape(Bsz, T, MB_WIDTH) * jax.nn.silu(z)).reshape(Bsz, T, MB_GROUPS, MB_WIDTH // MB_GROUPS)
    y = y * lax.rsqrt(jnp.mean(y * y, axis=-1, keepdims=True) + MB_NORM_EPS)
    return (y.reshape(Bsz, T, MB_WIDTH) * norm_w.astype(F32)).astype(dt_in)


def mlstm_chunkwise(q, k, v, log_i, log_f):
    Bsz, T, H, DK = q.shape
    DV = v.shape[-1]
    nc = T // CHUNK
    q = q.reshape(Bsz, nc, CHUNK, H, DK)
    k = k.reshape(Bsz, nc, CHUNK, H, DK)
    v = v.reshape(Bsz, nc, CHUNK, H, DV)
    li = log_i.reshape(Bsz, nc, CHUNK, H)
    b = jnp.cumsum(log_f.reshape(Bsz, nc, CHUNK, H), axis=2)
    b_last = b[:, :, -1]
    g_end = b_last[:, :, None] - b + li
    m_loc = jnp.max(g_end, axis=2)
    w_end = jnp.exp(g_end - m_loc[:, :, None])
    C_loc = jnp.einsum('bcsh,bcshv,bcshk->bchvk', w_end, v, k)
    n_loc = jnp.einsum('bcsh,bcshk->bchk', w_end, k)

    def step(carry, inp):
        C, n, m = carry
        Cl, nl, ml, bl = inp
        m_new = jnp.maximum(bl + m, ml)
        s_old = jnp.exp(bl + m - m_new)
        s_loc = jnp.exp(ml - m_new)
        C_new = C * s_old[..., None, None] + Cl * s_loc[..., None, None]
        n_new = n * s_old[..., None] + nl * s_loc[..., None]
        return (C_new, n_new, m_new), (C, n, m)
    init = (jnp.zeros((Bsz, H, DV, DK), F32), jnp.zeros((Bsz, H, DK), F32), jnp.zeros((Bsz, H), F32))
    _, (C_prev, n_prev, m_prev) = lax.scan(
        step, init, tuple(jnp.moveaxis(t, 1, 0) for t in (C_loc, n_loc, m_loc, b_last)))
    C_prev = jnp.moveaxis(C_prev, 0, 1)
    n_prev = jnp.moveaxis(n_prev, 0, 1)
    m_prev = jnp.moveaxis(m_prev, 0, 1)
    inter = b + m_prev[:, :, None]
    Dlog = b[:, :, :, None, :] - b[:, :, None, :, :] + li[:, :, None, :, :]
    causal = jnp.tril(jnp.ones((CHUNK, CHUNK), dtype=bool))[None, None, :, :, None]
    Dlog = jnp.where(causal, Dlog, -jnp.inf)
    m_t = jnp.maximum(inter, jnp.max(Dlog, axis=3))
    S = jnp.einsum('bcthk,bcshk->bctsh', q, k) * jnp.exp(Dlog - m_t[:, :, :, None, :])
    scale_inter = jnp.exp(inter - m_t)
    num = jnp.einsum('bctsh,bcshv->bcthv', S, v) \
        + scale_inter[..., None] * jnp.einsum('bchvk,bcthk->bcthv', C_prev, q)
    den = jnp.sum(S, axis=3) + scale_inter * jnp.einsum('bchk,bcthk->bcth', n_prev, q)
    h = num / jnp.maximum(jnp.abs(den), jnp.exp(-m_t))[..., None]
    return h.reshape(Bsz, T, H, DV)


def mlstm_mix(p, b_gates, norm_w):
    dt_in = p.dtype
    p = p.astype(F32)
    Bsz, T, _ = p.shape
    q, k, v, o, gi, gf = jnp.split(p, [ML_QK_WIDTH, 2 * ML_QK_WIDTH, 2 * ML_QK_WIDTH + ML_V_WIDTH,
                                       2 * ML_QK_WIDTH + 2 * ML_V_WIDTH,
                                       2 * ML_QK_WIDTH + 2 * ML_V_WIDTH + ML_HEADS], axis=-1)
    q = q.reshape(Bsz, T, ML_HEADS, ML_DQK)
    k = k.reshape(Bsz, T, ML_HEADS, ML_DQK) * (ML_DQK ** -0.5)
    v = v.reshape(Bsz, T, ML_HEADS, ML_DV)
    bg = b_gates.astype(F32)
    log_i = softcap(gi + bg[:ML_HEADS])
    log_f = jax.nn.log_sigmoid(softcap(gf + bg[ML_HEADS:]))
    h = mlstm_chunkwise(q, k, v, log_i, log_f)
    h = h * lax.rsqrt(jnp.mean(h * h, axis=-1, keepdims=True) + NORM_EPS) \
        * norm_w.astype(F32).reshape(ML_HEADS, ML_DV)
    return (h.reshape(Bsz, T, ML_V_WIDTH) * jax.nn.sigmoid(o)).astype(dt_in)


def memory_cross_attention(h, mem_n, wq, wk, wv, wo):
    Bsz, T, _ = h.shape
    M = mem_n.shape[1]
    q = (h @ wq).reshape(Bsz, T, XA_HEADS, XA_HEAD)
    k = (mem_n @ wk).reshape(Bsz, M, XA_HEADS, XA_HEAD)
    v = (mem_n @ wv).reshape(Bsz, M, XA_HEADS, XA_HEAD)
    s = jnp.einsum('bthd,bmhd->bhtm', q, k).astype(F32) * (XA_HEAD ** -0.5)
    pr = jax.nn.softmax(s, axis=-1).astype(v.dtype)
    o = jnp.einsum('bhtm,bmhd->bthd', pr, v).reshape(Bsz, T, D_MODEL)
    return o @ wo


def squared_relu_mlp(h, w_up, w_down):
    return jnp.square(jax.nn.relu(h @ w_up)) @ w_down


def setup_inputs(seed: int = 0) -> dict:
    key = jax.random.key(seed)
    ks = iter(jax.random.split(key, 64))
    nrm = lambda shape, scale: jax.random.normal(next(ks), shape, F32) * scale
    gain = lambda shape: 1.0 + nrm(shape, 0.02)
    x = nrm((BATCH, SEQ, D_MODEL), 1.0)
    mem = nrm((BATCH, N_MEM, D_MODEL), 1.0)
    w0_base = -6.0 + 5.0 * (jnp.arange(RW_WIDTH, dtype=F32) / (RW_WIDTH - 1)) ** 0.85
    dt0 = jnp.exp(jax.random.uniform(next(ks), (N_EVEN, MB_HEADS), F32,
                                     minval=math.log(1e-3), maxval=math.log(1e-1)))
    A0 = jax.random.uniform(next(ks), (N_EVEN, MB_HEADS), F32, minval=1.0, maxval=16.0)
    b_i = nrm((N_ODD, ML_HEADS), 0.1)
    b_f = jnp.linspace(3.0, 6.0, ML_HEADS, dtype=F32)[None] + nrm((N_ODD, ML_HEADS), 0.1)
    return {
        'x': x,
        'mem': mem,
        'norm_mix_pre': gain((DEPTH, D_MODEL)),
        'norm_mix_post': gain((DEPTH, D_MODEL)),
        'norm_xa_pre': gain((DEPTH, D_MODEL)),
        'norm_xa_post': gain((DEPTH, D_MODEL)),
        'norm_mem': gain((DEPTH, D_MODEL)),
        'norm_ff_pre': gain((DEPTH, D_MODEL)),
        'norm_ff_post': gain((DEPTH, D_MODEL)),
        'xa_wq': nrm((DEPTH, D_MODEL, D_MODEL), D_MODEL ** -0.5),
        'xa_wk': nrm((DEPTH, D_MODEL, D_MODEL), D_MODEL ** -0.5),
        'xa_wv': nrm((DEPTH, D_MODEL, D_MODEL), D_MODEL ** -0.5),
        'xa_wo': nrm((DEPTH, D_MODEL, D_MODEL), D_MODEL ** -0.5),
        'ff_up': nrm((DEPTH, D_MODEL, D_FF), D_MODEL ** -0.5),
        'ff_down': nrm((DEPTH, D_FF, D_MODEL), D_FF ** -0.5),
        'ev_w_in': nrm((N_EVEN, D_MODEL, EVEN_COLS), D_MODEL ** -0.5),
        'ev_w_out': nrm((N_EVEN, EVEN_OUT, D_MODEL), EVEN_OUT ** -0.5),
        'rw_mu': jax.random.uniform(next(ks), (N_EVEN, RW_COLS), F32),
        'rw_w0': w0_base[None] + nrm((N_EVEN, RW_WIDTH), 0.1),
        'rw_w2': nrm((N_EVEN, RW_DECAY_LORA, RW_WIDTH), 0.5 * RW_DECAY_LORA ** -0.5),
        'rw_a0': nrm((N_EVEN, RW_WIDTH), 0.1),
        'rw_a2': nrm((N_EVEN, RW_AAA_LORA, RW_WIDTH), 0.5 * RW_AAA_LORA ** -0.5),
        'rw_g2': nrm((N_EVEN, RW_GATE_LORA, RW_WIDTH), RW_GATE_LORA ** -0.5),
        'rw_k_k': 0.85 + nrm((N_EVEN, RW_WIDTH), 0.02),
        'rw_k_a': 1.0 + nrm((N_EVEN, RW_WIDTH), 0.02),
        'rw_r_k': -0.04 + nrm((N_EVEN, RW_HEADS, RW_HEAD), 0.1),
        'rw_ln_w': gain((N_EVEN, RW_WIDTH)),
        'rw_ln_b': nrm((N_EVEN, RW_WIDTH), 0.02),
        'mb_conv_w': nrm((N_EVEN, MB_CONV, MB_CONV_CH), MB_CONV ** -0.5),
        'mb_conv_b': nrm((N_EVEN, MB_CONV_CH), 0.02),
        'mb_dt_bias': dt0 + jnp.log(-jnp.expm1(-dt0)),
        'mb_A_log': jnp.log(A0),
        'mb_D': 1.0 + nrm((N_EVEN, MB_HEADS), 0.05),
        'mb_norm_w': gain((N_EVEN, MB_WIDTH)),
        'ml_w_in': nrm((N_ODD, D_MODEL, ODD_COLS), D_MODEL ** -0.5),
        'ml_w_out': nrm((N_ODD, ML_V_WIDTH, D_MODEL), ML_V_WIDTH ** -0.5),
        'ml_b_gates': jnp.concatenate([b_i, b_f], axis=-1),
        'ml_norm_w': gain((N_ODD, ML_V_WIDTH)),
    }


def reference(x, mem, norm_mix_pre, norm_mix_post, norm_xa_pre, norm_xa_post, norm_mem,
              norm_ff_pre, norm_ff_post, xa_wq, xa_wk, xa_wv, xa_wo, ff_up, ff_down,
              ev_w_in, ev_w_out, rw_mu, rw_w0, rw_w2, rw_a0, rw_a2, rw_g2, rw_k_k, rw_k_a,
              rw_r_k, rw_ln_w, rw_ln_b, mb_conv_w, mb_conv_b, mb_dt_bias, mb_A_log, mb_D,
              mb_norm_w, ml_w_in, ml_w_out, ml_b_gates, ml_norm_w):
    for i in range(DEPTH):
        j = i // 2
        hn = rmsnorm(x, norm_mix_pre[i])
        if i % 2 == 0:
            proj = hn @ ev_w_in[j]
            ya = rwkv7_mix(proj[..., :RW_COLS], rw_mu[j], rw_w0[j], rw_w2[j], rw_a0[j], rw_a2[j],
                           rw_g2[j], rw_k_k[j], rw_k_a[j], rw_r_k[j], rw_ln_w[j], rw_ln_b[j])
            yb = mamba2_mix(proj[..., RW_COLS:], mb_conv_w[j], mb_conv_b[j], mb_dt_bias[j],
                            mb_A_log[j], mb_D[j], mb_norm_w[j])
            mix = jnp.concatenate([ya, yb], axis=-1) @ ev_w_out[j]
        else:
            proj = hn @ ml_w_in[j]
            mix = mlstm_mix(proj, ml_b_gates[j], ml_norm_w[j]) @ ml_w_out[j]
        x = x + rmsnorm(mix, norm_mix_post[i])
        mem_n = rmsnorm(mem, norm_mem[i])
        ca = memory_cross_attention(rmsnorm(x, norm_xa_pre[i]), mem_n,
                                    xa_wq[i], xa_wk[i], xa_wv[i], xa_wo[i])
        x = x + rmsnorm(ca, norm_xa_post[i])
        ff = squared_relu_mlp(rmsnorm(x, norm_ff_pre[i]), ff_up[i], ff_down[i])
        x = x + rmsnorm(ff, norm_ff_post[i])
    return x
```

```python
import functools

import jax
import jax.numpy as jnp
from jax import lax
from jax.experimental import pallas as pl
from jax.experimental.pallas import tpu as pltpu

F32 = jnp.float32
BF16 = jnp.bfloat16

D_MODEL = 1024
NORM_EPS = 1e-6

RW_WIDTH = 512
RW_HEAD = 64
RW_HEADS = RW_WIDTH // RW_HEAD
RW_DECAY_LORA = 64
RW_AAA_LORA = 64
RW_GATE_LORA = 128
RW_COLS = 3 * RW_WIDTH + RW_DECAY_LORA + RW_AAA_LORA + RW_GATE_LORA
RW_LN_EPS = 64e-5

MB_WIDTH = 512
MB_HEAD = 64
MB_HEADS = MB_WIDTH // MB_HEAD
MB_GROUPS = 2
MB_STATE = 128
MB_CONV = 4
MB_BC = MB_GROUPS * MB_STATE
MB_CONV_CH = MB_WIDTH + 2 * MB_BC
MB_NORM_EPS = 1e-5

ML_HEADS = 8
ML_V_WIDTH = 1024
ML_QK_WIDTH = 512
ML_DV = ML_V_WIDTH // ML_HEADS
ML_DQK = ML_QK_WIDTH // ML_HEADS
ML_GATE_CAP = 15.0

XA_HEADS = 4
XA_HEAD = D_MODEL // XA_HEADS
D_FF = 4 * D_MODEL

LANES = 128
SUBLANES = 8
VMEM_LIMIT_BYTES = 56 * 1024 * 1024

ROW_TILE = 256
RW_CHUNK = 64
RW_TILE = 128
SSD_CHUNK = 256
ML_CHUNK = 256
FF_CHUNK = 1024


def _cparams(sem):
    return pltpu.CompilerParams(dimension_semantics=sem, vmem_limit_bytes=VMEM_LIMIT_BYTES)


def _dot(a, b):
    return jnp.dot(a, b, preferred_element_type=F32)


def _dot_nt(a, b):
    return lax.dot_general(a, b, (((1,), (1,)), ((), ())), preferred_element_type=F32)


def _dot_tn(a, b):
    return lax.dot_general(a, b, (((0,), (0,)), ((), ())), preferred_element_type=F32)


def _split_dot(x, w_bf16, terms):
    acc = None
    rem = x
    for _ in range(terms):
        piece = rem.astype(BF16)
        rem = rem - piece.astype(F32)
        d = _dot(piece, w_bf16)
        acc = d if acc is None else acc + d
    return acc


def _rms(x, g, eps):
    return x * lax.rsqrt(jnp.mean(x * x, axis=-1, keepdims=True) + eps) * g


def _softplus(x):
    return jnp.maximum(x, 0.0) + jnp.log1p(jnp.exp(-jnp.abs(x)))


def _sigmoid(x):
    return 1.0 / (1.0 + jnp.exp(-x))


def _tril_ones(n, dtype):
    r = lax.broadcasted_iota(jnp.int32, (n, n), 0)
    c = lax.broadcasted_iota(jnp.int32, (n, n), 1)
    return (r >= c).astype(dtype)


def _norm_matmul_body(x_ref, g_ref, w_ref, *o_refs, splits):
    hn = _rms(x_ref[...], g_ref[...], NORM_EPS).astype(BF16)
    off = 0
    for o_ref, n in zip(o_refs, splits):
        o_ref[...] = _dot(hn, w_ref[:, off:off + n])
        off += n


def _norm_matmul(x2d, g, w_bf16, splits, tm=ROW_TILE):
    m, d = x2d.shape
    n_tot = w_bf16.shape[1]
    assert sum(splits) == n_tot and m % tm == 0
    return pl.pallas_call(
        functools.partial(_norm_matmul_body, splits=splits),
        grid=(m // tm,),
        in_specs=[pl.BlockSpec((tm, d), lambda i: (i, 0)),
                  pl.BlockSpec((1, d), lambda i: (0, 0)),
                  pl.BlockSpec((d, n_tot), lambda i: (0, 0))],
        out_specs=[pl.BlockSpec((tm, n), lambda i: (i, 0)) for n in splits],
        out_shape=[jax.ShapeDtypeStruct((m, n), F32) for n in splits],
        compiler_params=_cparams(("parallel",)),
        name="norm_matmul",
    )(x2d, g.reshape(1, d), w_bf16)


def _rwkv_body(p_ref, mu_ref, w0_ref, w2_ref, a0_ref, a2_ref, g2_ref, kk_ref, ka_ref, rk_ref,
               lnw_ref, lnb_ref, ones_ref, o_ref, prev_ref, s_ref, y_ref, *, tt):
    @pl.when(pl.program_id(1) == 0)
    def _():
        prev_ref[...] = jnp.zeros_like(prev_ref)
        s_ref[...] = jnp.zeros_like(s_ref)

    p = p_ref[...]
    row = lax.broadcasted_iota(jnp.int32, p.shape, 0)
    shifted = jnp.where(row == 0, prev_ref[...], pltpu.roll(p, 1, 0))
    prev_ref[...] = p[tt - 1:tt, :]
    p = p + (shifted - p) * mu_ref[...]

    r = p[:, 0:RW_WIDTH]
    k = p[:, RW_WIDTH:2 * RW_WIDTH]
    v = p[:, 2 * RW_WIDTH:3 * RW_WIDTH]
    o0 = 3 * RW_WIDTH
    wd = jnp.tanh(p[:, o0:o0 + RW_DECAY_LORA])
    ad = p[:, o0 + RW_DECAY_LORA:o0 + RW_DECAY_LORA + RW_AAA_LORA]
    gd = _sigmoid(p[:, o0 + RW_DECAY_LORA + RW_AAA_LORA:RW_COLS])

    w_log = -_softplus(-(w0_ref[...] + _dot(wd, w2_ref[...]))) - 0.5
    logdec = -jnp.exp(w_log)
    a = _sigmoid(a0_ref[...] + _dot(ad, a2_ref[...]))
    g = _dot(gd, g2_ref[...])

    ones = ones_ref[...]
    kk = k * kk_ref[...]
    kk = kk / jnp.maximum(jnp.sqrt(_split_dot(kk * kk, ones, 2)), 1e-12)
    k = k * (1.0 + (a - 1.0) * ka_ref[...])
    av = -kk
    bv = kk * a

    L = RW_CHUNK
    tril = _tril_ones(L, BF16)
    r2 = lax.broadcasted_iota(jnp.int32, (L, 2 * L), 0)
    c2 = lax.broadcasted_iota(jnp.int32, (L, 2 * L), 1) & (L - 1)
    strict2 = r2 > c2
    incl2 = r2 >= c2

    for c in range(tt // L):
        rows = slice(c * L, (c + 1) * L)
        ld = logdec[rows, :]
        cum = _split_dot_lhs_tri(tril, ld)
        e_incl = jnp.exp(cum)
        e_excl = jnp.exp(cum - ld)
        e_inv = jnp.exp(-cum)
        cum_last = cum[L - 1:L, :]
        e_end = jnp.exp(cum_last - cum)
        w_last = jnp.exp(cum_last)
        at = av[rows, :] * e_excl
        rt = r[rows, :] * e_incl
        bt = bv[rows, :] * e_inv
        kt = k[rows, :] * e_inv
        bh = bv[rows, :] * e_end
        kh = k[rows, :] * e_end
        vc = v[rows, :]
        for h in range(RW_HEADS):
            ln = slice(h * RW_HEAD, (h + 1) * RW_HEAD)
            s0 = s_ref[h]
            bk_t = jnp.concatenate([bt[:, ln], kt[:, ln]], axis=0)
            a_all = jnp.where(strict2, _dot_nt(at[:, ln], bk_t), 0.0)
            r_all = jnp.where(incl2, _dot_nt(rt[:, ln], bk_t), 0.0)
            n_mat = a_all[:, 0:L]
            vh = vc[:, ln]
            u = _dot_nt(at[:, ln], s0) + _dot(a_all[:, L:2 * L], vh)
            u = u + _dot(n_mat, u)
            n_pow = n_mat
            for _ in range(5):
                n_pow = _dot(n_pow, n_pow)
                u = u + _dot(n_pow, u)
            uv = jnp.concatenate([u, vh], axis=0)
            y_ref[rows, ln] = _dot_nt(rt[:, ln], s0) + _dot(r_all, uv)
            bk_h = jnp.concatenate([bh[:, ln], kh[:, ln]], axis=0)
            s_ref[h] = s0 * w_last[:, ln] + _dot_tn(uv, bk_h)

    y = y_ref[...]
    inv_n = 1.0 / RW_HEAD
    mean = _split_dot(y, ones, 2) * inv_n
    yc = y - mean
    var = _split_dot(yc * yc, ones, 2) * inv_n
    y = yc * lax.rsqrt(var + RW_LN_EPS) * lnw_ref[...] + lnb_ref[...]
    y = y + _split_dot(r * k * rk_ref[...], ones, 2) * v
    o_ref[...] = y * g


def _split_dot_lhs_tri(tri_bf16, x):
    acc = None
    rem = x
    for _ in range(3):
        piece = rem.astype(BF16)
        rem = rem - piece.astype(F32)
        d = _dot(tri_bf16, piece)
        acc = d if acc is None else acc + d
    return acc


def _rwkv7(p_rw, mu, w0, w2, a0, a2, g2, k_k, k_a, r_k, ln_w, ln_b, tt=RW_TILE):
    b, t, _ = p_rw.shape
    assert t % tt == 0 and tt % RW_CHUNK == 0
    hid = jnp.arange(RW_WIDTH, dtype=jnp.int32) // RW_HEAD
    ones = (hid[:, None] == hid[None, :]).astype(BF16)
    row = lambda x: x.reshape(1, -1).astype(F32)
    vec = lambda n: pl.BlockSpec((1, n), lambda bi, ti: (0, 0))
    mat = lambda m, n: pl.BlockSpec((m, n), lambda bi, ti: (0, 0))
    return pl.pallas_call(
        functools.partial(_rwkv_body, tt=tt),
        grid=(b, t // tt),
        in_specs=[pl.BlockSpec((None, tt, RW_COLS), lambda bi, ti: (bi, ti, 0)),
                  vec(RW_COLS), vec(RW_WIDTH), mat(RW_DECAY_LORA, RW_WIDTH), vec(RW_WIDTH),
                  mat(RW_AAA_LORA, RW_WIDTH), mat(RW_GATE_LORA, RW_WIDTH), vec(RW_WIDTH), vec(RW_WIDTH),
                  vec(RW_WIDTH), vec(RW_WIDTH), vec(RW_WIDTH), mat(RW_WIDTH, RW_WIDTH)],
        out_specs=pl.BlockSpec((None, tt, RW_WIDTH), lambda bi, ti: (bi, ti, 0)),
        out_shape=jax.ShapeDtypeStruct((b, t, RW_WIDTH), F32),
        scratch_shapes=[pltpu.VMEM((1, RW_COLS), F32),
                        pltpu.VMEM((RW_HEADS, RW_HEAD, RW_HEAD), F32),
                        pltpu.VMEM((tt, RW_WIDTH), F32)],
        compiler_params=_cparams(("parallel", "arbitrary")),
        name="rwkv7_mix",
    )(p_rw, row(mu), row(w0), w2.astype(F32), row(a0), a2.astype(F32), g2.astype(F32), row(k_k), row(k_a),
      row(r_k), row(ln_w), row(ln_b), ones)


def _mamba_body(z_ref, xbc_ref, dt_ref, cw_ref, cb_ref, dtb_ref, alog_ref, dskip_ref, nw_ref, exp_ref,
                o_ref, tail_ref, st_ref, y_ref, *, lc):
    @pl.when(pl.program_id(1) == 0)
    def _():
        tail_ref[...] = jnp.zeros_like(tail_ref)
        st_ref[...] = jnp.zeros_like(st_ref)

    raw = xbc_ref[...]
    tail = tail_ref[...]
    tail_ref[...] = raw[lc - SUBLANES:lc, :]
    row8 = lax.broadcasted_iota(jnp.int32, (SUBLANES, MB_CONV_CH), 0)
    cw = cw_ref[...]
    acc = raw * cw[MB_CONV - 1:MB_CONV, :] + cb_ref[...]
    for j in range(1, MB_CONV):
        xj = pltpu.roll(raw, j, 0)
        head = jnp.where(row8 < j, pltpu.roll(tail, j, 0), xj[0:SUBLANES, :])
        xj = jnp.concatenate([head, xj[SUBLANES:, :]], axis=0)
        acc = acc + xj * cw[MB_CONV - 1 - j:MB_CONV - j, :]
    xbc = acc * _sigmoid(acc)
    xs = xbc[:, 0:MB_WIDTH]
    bm = xbc[:, MB_WIDTH:MB_WIDTH + MB_BC]
    cm = xbc[:, MB_WIDTH + MB_BC:MB_CONV_CH]

    dt = _softplus(dt_ref[...] + dtb_ref[...])
    da = dt * (-jnp.exp(alog_ref[...]))
    acum = _split_dot_lhs_tri(_tril_ones(lc, BF16), da)
    acum_t = acum.T
    expand = exp_ref[...]
    acum_f = _split_dot(acum, expand, 3)
    dt_f = _split_dot(dt, expand, 3)
    last_f = acum_f[lc - 1:lc, :]
    xdt = xs * dt_f
    xw = xdt * jnp.exp(last_f - acum_f)
    e_in = jnp.exp(acum_f)
    e_last = jnp.exp(last_f)

    ri = lax.broadcasted_iota(jnp.int32, (lc, lc), 0)
    ci = lax.broadcasted_iota(jnp.int32, (lc, lc), 1)
    causal = ri >= ci
    gw = MB_WIDTH // MB_GROUPS
    for gi in range(MB_GROUPS):
        gl = slice(gi * gw, (gi + 1) * gw)
        bg = bm[:, gi * MB_STATE:(gi + 1) * MB_STATE]
        cg = cm[:, gi * MB_STATE:(gi + 1) * MB_STATE]
        cb = _dot_nt(cg, bg)
        st = st_ref[gi]
        y_ref[:, gl] = _dot(cg, st) * e_in[:, gl]
        st_ref[gi] = st * e_last[:, gl] + _dot_tn(bg, xw[:, gl])
        for hj in range(MB_HEADS // MB_GROUPS):
            h = gi * (MB_HEADS // MB_GROUPS) + hj
            hl = slice(h * MB_HEAD, (h + 1) * MB_HEAD)
            seg = acum[:, h:h + 1] - acum_t[h:h + 1, :]
            lmat = jnp.exp(jnp.where(causal, seg, -jnp.inf))
            y_ref[:, hl] += _dot(cb * lmat, xdt[:, hl])

    y = y_ref[...] + dskip_ref[...] * xs
    z = z_ref[...]
    y = y * (z * _sigmoid(z))
    parts = []
    for gi in range(MB_GROUPS):
        yg = y[:, gi * gw:(gi + 1) * gw]
        parts.append(yg * lax.rsqrt(jnp.mean(yg * yg, axis=-1, keepdims=True) + MB_NORM_EPS))
    o_ref[...] = jnp.concatenate(parts, axis=-1) * nw_ref[...]


def _mamba2(z, xbc, dt, conv_w, conv_b, dt_bias, a_log, d_skip, norm_w, lc=SSD_CHUNK):
    b, t, _ = z.shape
    assert t % lc == 0
    pad = lambda x: jnp.pad(x.astype(F32), (0, LANES - MB_HEADS)).reshape(1, LANES)
    hid = jnp.arange(MB_WIDTH, dtype=jnp.int32) // MB_HEAD
    expand = (jnp.arange(LANES, dtype=jnp.int32)[:, None] == hid[None, :]).astype(BF16)
    d_full = jnp.repeat(d_skip.astype(F32), MB_HEAD).reshape(1, MB_WIDTH)
    vec = lambda n: pl.BlockSpec((1, n), lambda bi, ti: (0, 0))
    mat = lambda m, n: pl.BlockSpec((m, n), lambda bi, ti: (0, 0))
    tile = lambda n: pl.BlockSpec((None, lc, n), lambda bi, ti: (bi, ti, 0))
    return pl.pallas_call(
        functools.partial(_mamba_body, lc=lc),
        grid=(b, t // lc),
        in_specs=[tile(MB_WIDTH), tile(MB_CONV_CH), tile(LANES),
                  mat(MB_CONV, MB_CONV_CH), vec(MB_CONV_CH), vec(LANES), vec(LANES), vec(MB_WIDTH),
                  vec(MB_WIDTH), mat(LANES, MB_WIDTH)],
        out_specs=tile(MB_WIDTH),
        out_shape=jax.ShapeDtypeStruct((b, t, MB_WIDTH), F32),
        scratch_shapes=[pltpu.VMEM((SUBLANES, MB_CONV_CH), F32),
                        pltpu.VMEM((MB_GROUPS, MB_STATE, MB_WIDTH // MB_GROUPS), F32),
                        pltpu.VMEM((lc, MB_WIDTH), F32)],
        compiler_params=_cparams(("parallel", "arbitrary")),
        name="mamba2_mix",
    )(z, xbc, dt, conv_w.astype(F32), conv_b.reshape(1, -1).astype(F32), pad(dt_bias), pad(a_log), d_full,
      norm_w.reshape(1, -1).astype(F32), expand)


def _mlstm_body(qk_ref, v_ref, og_ref, gt_ref, bg_ref, nw_ref, o_ref, ct_ref, *, lc):
    @pl.when(pl.program_id(1) == 0)
    def _():
        ct_ref[...] = jnp.zeros_like(ct_ref)

    gates = ML_GATE_CAP * jnp.tanh((gt_ref[...] + bg_ref[...]) * (1.0 / ML_GATE_CAP))
    logf = jnp.minimum(gates, 0.0) - jnp.log1p(jnp.exp(-jnp.abs(gates)))
    bcum = _split_dot_lhs_tri(_tril_ones(lc, BF16), logf)
    gates_t = gates.T
    bcum_t = bcum.T

    ri = lax.broadcasted_iota(jnp.int32, (lc, lc), 0)
    ci = lax.broadcasted_iota(jnp.int32, (lc, lc), 1)
    causal = ri >= ci
    lane = lax.broadcasted_iota(jnp.int32, (lc, ML_DV), 1)
    ones_col = (lane == 0).astype(F32)
    scale = ML_DQK ** -0.5

    for h in range(ML_HEADS):
        q = qk_ref[:, h * ML_DQK:(h + 1) * ML_DQK]
        k = qk_ref[:, ML_QK_WIDTH + h * ML_DQK:ML_QK_WIDTH + (h + 1) * ML_DQK] * scale
        v_ext = jnp.concatenate([v_ref[:, h * ML_DV:(h + 1) * ML_DV], ones_col], axis=-1)
        b_col = bcum[:, ML_HEADS + h:ML_HEADS + h + 1]
        i_col = gates[:, h:h + 1]
        src_row = gates_t[h:h + 1, :] - bcum_t[ML_HEADS + h:ML_HEADS + h + 1, :]
        dmat = jnp.exp(jnp.where(causal, b_col + src_row, -jnp.inf))
        s = _dot_nt(q, k) * dmat
        ct = ct_ref[h]
        res = _dot(s, v_ext) + jnp.exp(b_col) * _dot(q, ct)
        den = res[:, ML_DV:ML_DV + 1]
        hv = res[:, 0:ML_DV] / jnp.maximum(jnp.abs(den), 1.0)
        b_last = b_col[lc - 1:lc, :]
        kw = k * jnp.exp(b_last - b_col + i_col)
        ct_ref[h] = ct * jnp.exp(b_last) + _dot_tn(kw, v_ext)
        hl = slice(h * ML_DV, (h + 1) * ML_DV)
        hn = hv * lax.rsqrt(jnp.mean(hv * hv, axis=-1, keepdims=True) + NORM_EPS) * nw_ref[:, hl]
        o_ref[:, hl] = hn * _sigmoid(og_ref[:, hl])


def _mlstm(qk, v, og, gt, b_gates, norm_w, lc=ML_CHUNK):
    b, t, _ = qk.shape
    assert t % lc == 0
    bg = jnp.pad(b_gates.astype(F32), (0, LANES - 2 * ML_HEADS)).reshape(1, LANES)
    vec = lambda n: pl.BlockSpec((1, n), lambda bi, ti: (0, 0))
    tile = lambda n: pl.BlockSpec((None, lc, n), lambda bi, ti: (bi, ti, 0))
    return pl.pallas_call(
        functools.partial(_mlstm_body, lc=lc),
        grid=(b, t // lc),
        in_specs=[tile(2 * ML_QK_WIDTH), tile(ML_V_WIDTH), tile(ML_V_WIDTH), tile(LANES), vec(LANES),
                  vec(ML_V_WIDTH)],
        out_specs=tile(ML_V_WIDTH),
        out_shape=jax.ShapeDtypeStruct((b, t, ML_V_WIDTH), F32),
        scratch_shapes=[pltpu.VMEM((ML_HEADS, ML_DQK, 2 * ML_DV), F32)],
        compiler_params=_cparams(("parallel", "arbitrary")),
        name="mlstm_mix",
    )(qk, v, og, gt, bg, norm_w.reshape(1, -1).astype(F32))


def _out_proj_body(*refs, n_parts):
    x_ref = refs[0]
    y_refs = refs[1:1 + n_parts]
    w_refs = refs[1 + n_parts:1 + 2 * n_parts]
    g_ref = refs[1 + 2 * n_parts]
    o_ref = refs[2 + 2 * n_parts]
    acc = None
    for y_ref, w_ref in zip(y_refs, w_refs):
        d = _dot(y_ref[...].astype(BF16), w_ref[...])
        acc = d if acc is None else acc + d
    o_ref[...] = x_ref[...] + _rms(acc, g_ref[...], NORM_EPS)


def _out_proj(x2d, parts, w_parts_bf16, g, tm=ROW_TILE):
    m, d = x2d.shape
    n_parts = len(parts)
    row = lambda n: pl.BlockSpec((tm, n), lambda i: (i, 0))
    full = lambda a: pl.BlockSpec(a.shape, lambda i: (0, 0))
    return pl.pallas_call(
        functools.partial(_out_proj_body, n_parts=n_parts),
        grid=(m // tm,),
        in_specs=[row(d)] + [row(p.shape[1]) for p in parts] + [full(w) for w in w_parts_bf16]
                 + [pl.BlockSpec((1, d), lambda i: (0, 0))],
        out_specs=row(d),
        out_shape=jax.ShapeDtypeStruct((m, d), F32),
        compiler_params=_cparams(("parallel",)),
        name="mix_out_proj",
    )(x2d, *parts, *w_parts_bf16, g.reshape(1, d).astype(F32))


def _xattn_body(x_ref, k_ref, v_ref, gpre_ref, wq_ref, wo_ref, gpost_ref, o_ref):
    x = x_ref[...]
    hn = _rms(x, gpre_ref[...], NORM_EPS).astype(BF16)
    q = _dot(hn, wq_ref[...])
    outs = []
    for h in range(XA_HEADS):
        hl = slice(h * XA_HEAD, (h + 1) * XA_HEAD)
        s = _dot_nt(q[:, hl].astype(BF16), k_ref[:, hl]) * (XA_HEAD ** -0.5)
        s = s - jnp.max(s, axis=-1, keepdims=True)
        e = jnp.exp(s)
        pr = e / jnp.sum(e, axis=-1, keepdims=True)
        outs.append(_dot(pr.astype(BF16), v_ref[:, hl]))
    o = jnp.concatenate(outs, axis=-1).astype(BF16)
    ca = _dot(o, wo_ref[...])
    o_ref[...] = x + _rms(ca, gpost_ref[...], NORM_EPS)


def _xattn(x3d, k3d, v3d, g_pre, wq, wo, g_post, tm=ROW_TILE):
    b, t, d = x3d.shape
    nm = k3d.shape[1]
    vec = pl.BlockSpec((1, d), lambda bi, ti: (0, 0))
    wspec = pl.BlockSpec((d, d), lambda bi, ti: (0, 0))
    return pl.pallas_call(
        _xattn_body,
        grid=(b, t // tm),
        in_specs=[pl.BlockSpec((None, tm, d), lambda bi, ti: (bi, ti, 0)),
                  pl.BlockSpec((None, nm, d), lambda bi, ti: (bi, 0, 0)),
                  pl.BlockSpec((None, nm, d), lambda bi, ti: (bi, 0, 0)),
                  vec, wspec, wspec, vec],
        out_specs=pl.BlockSpec((None, tm, d), lambda bi, ti: (bi, ti, 0)),
        out_shape=jax.ShapeDtypeStruct((b, t, d), F32),
        compiler_params=_cparams(("parallel", "parallel")),
        name="mem_xattn",
    )(x3d, k3d, v3d, g_pre.reshape(1, d).astype(F32), wq, wo, g_post.reshape(1, d).astype(F32))


def _mlp_body(x_ref, gpre_ref, wu_ref, wd_ref, gpost_ref, o_ref):
    x = x_ref[...]
    hn = _rms(x, gpre_ref[...], NORM_EPS).astype(BF16)
    acc = None
    for c in range(D_FF // FF_CHUNK):
        cl = slice(c * FF_CHUNK, (c + 1) * FF_CHUNK)
        u = jnp.maximum(_dot(hn, wu_ref[:, cl]), 0.0)
        d = _dot((u * u).astype(BF16), wd_ref[cl, :])
        acc = d if acc is None else acc + d
    o_ref[...] = x + _rms(acc, gpost_ref[...], NORM_EPS)


def _mlp(x2d, g_pre, w_up, w_down, g_post, tm=ROW_TILE):
    m, d = x2d.shape
    vec = pl.BlockSpec((1, d), lambda i: (0, 0))
    return pl.pallas_call(
        _mlp_body,
        grid=(m // tm,),
        in_specs=[pl.BlockSpec((tm, d), lambda i: (i, 0)), vec,
                  pl.BlockSpec((d, D_FF), lambda i: (0, 0)),
                  pl.BlockSpec((D_FF, d), lambda i: (0, 0)), vec],
        out_specs=pl.BlockSpec((tm, d), lambda i: (i, 0)),
        out_shape=jax.ShapeDtypeStruct((m, d), F32),
        compiler_params=_cparams(("parallel",)),
        name="relu2_mlp",
    )(x2d, g_pre.reshape(1, d).astype(F32), w_up, w_down, g_post.reshape(1, d).astype(F32))


def _pad_cols(w, n):
    return jnp.pad(w, ((0, 0), (0, n - w.shape[1])))


def kernel(x, mem, norm_mix_pre, norm_mix_post, norm_xa_pre, norm_xa_post, norm_mem, norm_ff_pre, norm_ff_post, xa_wq, xa_wk, xa_wv, xa_wo, ff_up, ff_down, ev_w_in, ev_w_out, rw_mu, rw_w0, rw_w2, rw_a0, rw_a2, rw_g2, rw_k_k, rw_k_a, rw_r_k, rw_ln_w, rw_ln_b, mb_conv_w, mb_conv_b, mb_dt_bias, mb_A_log, mb_D, mb_norm_w, ml_w_in, ml_w_out, ml_b_gates, ml_norm_w):
    bsz, seq, d = x.shape
    n_mem = mem.shape[1]
    depth = norm_mix_pre.shape[0]
    rows = bsz * seq
    x2 = x.reshape(rows, d)
    mem2 = mem.reshape(bsz * n_mem, d)
    as3 = lambda a: a.reshape(bsz, seq, a.shape[-1])

    for i in range(depth):
        j = i // 2
        if i % 2 == 0:
            ev_cols = ev_w_in.shape[2]
            n_pad = -(-ev_cols // LANES) * LANES
            w_in = _pad_cols(ev_w_in[j], n_pad).astype(BF16)
            splits = (RW_COLS, MB_WIDTH, MB_CONV_CH, n_pad - RW_COLS - MB_WIDTH - MB_CONV_CH)
            p_rw, p_z, p_xbc, p_dt = _norm_matmul(x2, norm_mix_pre[i], w_in, splits)
            ya = _rwkv7(as3(p_rw), rw_mu[j], rw_w0[j], rw_w2[j], rw_a0[j], rw_a2[j], rw_g2[j], rw_k_k[j],
                        rw_k_a[j], rw_r_k[j], rw_ln_w[j], rw_ln_b[j])
            yb = _mamba2(as3(p_z), as3(p_xbc), as3(p_dt), mb_conv_w[j], mb_conv_b[j], mb_dt_bias[j],
                         mb_A_log[j], mb_D[j], mb_norm_w[j])
            w_out = ev_w_out[j].astype(BF16)
            x2 = _out_proj(x2, [ya.reshape(rows, RW_WIDTH), yb.reshape(rows, MB_WIDTH)],
                           [w_out[:RW_WIDTH], w_out[RW_WIDTH:]], norm_mix_post[i])
        else:
            odd_cols = ml_w_in.shape[2]
            n_pad = -(-odd_cols // LANES) * LANES
            w_in = _pad_cols(ml_w_in[j], n_pad).astype(BF16)
            splits = (2 * ML_QK_WIDTH, ML_V_WIDTH, ML_V_WIDTH, n_pad - 2 * ML_QK_WIDTH - 2 * ML_V_WIDTH)
            p_qk, p_v, p_o, p_g = _norm_matmul(x2, norm_mix_pre[i], w_in, splits)
            hm = _mlstm(as3(p_qk), as3(p_v), as3(p_o), as3(p_g), ml_b_gates[j], ml_norm_w[j])
            x2 = _out_proj(x2, [hm.reshape(rows, ML_V_WIDTH)], [ml_w_out[j].astype(BF16)], norm_mix_post[i])

        w_kv = jnp.concatenate([xa_wk[i], xa_wv[i]], axis=1).astype(BF16)
        mk, mv = _norm_matmul(mem2, norm_mem[i], w_kv, (d, d))
        x3 = _xattn(x2.reshape(bsz, seq, d), mk.astype(BF16).reshape(bsz, n_mem, d),
                    mv.astype(BF16).reshape(bsz, n_mem, d), norm_xa_pre[i], xa_wq[i].astype(BF16),
                    xa_wo[i].astype(BF16), norm_xa_post[i])
        x2 = _mlp(x3.reshape(rows, d), norm_ff_pre[i], ff_up[i].astype(BF16), ff_down[i].astype(BF16),
                  norm_ff_post[i])
    return x2.reshape(bsz, seq, d)
```
